```python
import jax, jax.numpy as jnp
from jax import lax
import numpy as np

D_MODEL = 2048
BATCH = 4
SEQ = 2048
DEPTH = 1
DEC_BATCH = 128
DEC_SEQ = 4
PAST_LEN = 16384
PAGE_SIZE = 128

LRU_WIDTH = D_MODEL // 2
POOL_WIDTH = D_MODEL - LRU_WIDTH
MIX_WIDTH = LRU_WIDTH + POOL_WIDTH
LRU_HEADS = 8
LRU_HEAD_DIM = LRU_WIDTH // LRU_HEADS
CONV_WIDTH = 4
LRU_C = 8.0
POOL_WINDOWS = (2, 4, 8, 16)
POOL_GROUPS = len(POOL_WINDOWS)
POOL_GROUP_DIM = POOL_WIDTH // POOL_GROUPS
POOL_HIST = max(POOL_WINDOWS) - 1
PEER_HEADS = 8
PEER_NKEYS = 128
PEER_EXPERTS = PEER_NKEYS * PEER_NKEYS
PEER_TOPK = 16
PEER_DKEY = 256
PEER_DHALF = PEER_DKEY // 2
PEER_BLOCK = 128
N_MOD = 6
EPS = 1e-6

kernel_name = 'hymba_rglru_pool_peer_adaln_step'


def rmsnorm(x, g):
    xf = x.astype(jnp.float32)
    y = xf * lax.rsqrt(jnp.mean(xf * xf, axis=-1, keepdims=True) + EPS)
    return (y * g.astype(jnp.float32)).astype(x.dtype)


def modulate(h, shift, scale):
    return h * (1 + scale[:, None]) + shift[:, None]


def causal_conv(xcat, w, b, T):
    out = xcat[:, CONV_WIDTH - 1:CONV_WIDTH - 1 + T] * w[CONV_WIDTH - 1] + b
    for k in range(CONV_WIDTH - 1):
        out = out + xcat[:, k:k + T] * w[k]
    return out


def rg_lru(xc, h_prev, wa, ba, wx, bx, lam, pos):
    B, T, C = xc.shape
    xf = xc.astype(jnp.float32)
    xh = xf.reshape(B, T, LRU_HEADS, LRU_HEAD_DIM)
    r = jax.nn.sigmoid(jnp.einsum('bthi,hij->bthj', xh, wa.astype(jnp.float32)) + ba.astype(jnp.float32)).reshape(B, T, C)
    i = jax.nn.sigmoid(jnp.einsum('bthi,hij->bthj', xh, wx.astype(jnp.float32)) + bx.astype(jnp.float32)).reshape(B, T, C)
    log_a = -LRU_C * r * jax.nn.softplus(-lam.astype(jnp.float32))
    a = jnp.exp(log_a)
    mult = jnp.sqrt(-jnp.expm1(2.0 * log_a))
    mult = jnp.where((pos == 0)[None, :, None], 1.0, mult)
    nx = xf * i * mult

    def step(h, inp):
        a_t, x_t = inp
        h = a_t * h + x_t
        return h, h

    h_last, hs = lax.scan(step, h_prev.astype(jnp.float32), (a.transpose(1, 0, 2), nx.transpose(1, 0, 2)))
    return hs.transpose(1, 0, 2), h_last


def pool_mix(pcat, pos, w, b, scale):
    B, L, C = pcat.shape
    T = L - POOL_HIST
    cs = jnp.cumsum(pcat.astype(jnp.float32), axis=1)
    cs = jnp.concatenate([jnp.zeros_like(cs[:, :1]), cs], axis=1)
    hi = np.arange(T) + POOL_HIST + 1
    outs = []
    for g, win in enumerate(POOL_WINDOWS):
        sl = slice(g * POOL_GROUP_DIM, (g + 1) * POOL_GROUP_DIM)
        s = cs[:, hi, sl] - cs[:, hi - win, sl]
        cnt = jnp.minimum(pos + 1, win).astype(jnp.float32)
        outs.append(s / cnt[None, :, None])
    pooled = jnp.concatenate(outs, axis=-1)
    d = (pooled - pcat[:, POOL_HIST:].astype(jnp.float32)).reshape(B, T, POOL_GROUPS, POOL_GROUP_DIM)
    y = jnp.einsum('btgi,gij->btgj', d, w.astype(jnp.float32)) + b.astype(jnp.float32)
    return (y.reshape(B, T, C) * scale.astype(jnp.float32)).astype(pcat.dtype)


def peer(h, wq, keys, eu, ev):
    T, D = h.shape
    Tp = -(-T // PEER_BLOCK) * PEER_BLOCK
    hp = jnp.pad(h, ((0, Tp - T), (0, 0)))
    q = (hp @ wq).astype(jnp.float32).reshape(Tp, PEER_HEADS, 2, PEER_DHALF)
    s = jnp.einsum('thpd,hpnd->thpn', q, keys.astype(jnp.float32))
    s1, i1 = lax.top_k(s[:, :, 0], PEER_TOPK)
    s2, i2 = lax.top_k(s[:, :, 1], PEER_TOPK)
    cand = (s1[..., :, None] + s2[..., None, :]).reshape(Tp, PEER_HEADS, PEER_TOPK * PEER_TOPK)
    cidx = (i1[..., :, None] * PEER_NKEYS + i2[..., None, :]).reshape(Tp, PEER_HEADS, PEER_TOPK * PEER_TOPK)
    top, sel = lax.top_k(cand, PEER_TOPK)
    eidx = jnp.take_along_axis(cidx, sel, axis=-1)
    gates = jax.nn.softmax(top, axis=-1)
    nb = Tp // PEER_BLOCK
    hb = hp.reshape(nb, PEER_BLOCK, D)
    eb = eidx.reshape(nb, PEER_BLOCK, PEER_HEADS * PEER_TOPK)
    gb = gates.reshape(nb, PEER_BLOCK, PEER_HEADS * PEER_TOPK).astype(h.dtype)

    def block(args):
        xb, ib, wb = args
        act = jax.nn.gelu(jnp.einsum('td,ted->te', xb, eu[ib]))
        return jnp.einsum('te,ted->td', wb * act, ev[ib])

    out = lax.map(block, (hb, eb, gb)).reshape(Tp, D)
    return out[:T]


def decoder_layer(x, c, conv_hist, h_prev, pool_hist, start_pos,
                  w_ada, b_ada, norm1_g, w_in, conv_w, conv_b, lru_wa, lru_ba, lru_wx, lru_bx,
                  lru_lambda, pool_w, pool_b, pool_scale, gnorm_lru_g, gnorm_pool_g, w_out,
                  norm2_g, peer_wq, peer_keys, peer_u, peer_v):
    B, T, D = x.shape
    pos = start_pos + jnp.arange(T, dtype=jnp.int32)
    mod = jax.nn.silu(c) @ w_ada + b_ada
    sh1, sc1, g1, sh2, sc2, g2 = jnp.split(mod, N_MOD, axis=-1)
    h = modulate(rmsnorm(x, norm1_g), sh1, sc1)
    z = h @ w_in
    u_lru = z[..., :LRU_WIDTH]
    u_gate = z[..., LRU_WIDTH:2 * LRU_WIDTH]
    u_pool = z[..., 2 * LRU_WIDTH:]
    conv_cat = jnp.concatenate([conv_hist.astype(x.dtype), u_lru], axis=1)
    xc = causal_conv(conv_cat, conv_w, conv_b, T)
    hs, h_last = rg_lru(xc, h_prev, lru_wa, lru_ba, lru_wx, lru_bx, lru_lambda, pos)
    y_lru = (hs * jax.nn.gelu(u_gate.astype(jnp.float32))).astype(x.dtype)
    pool_cat = jnp.concatenate([pool_hist.astype(x.dtype), u_pool], axis=1)
    y_pool = pool_mix(pool_cat, pos, pool_w, pool_b, pool_scale)
    mixed = jnp.concatenate([rmsnorm(y_lru, gnorm_lru_g), rmsnorm(y_pool, gnorm_pool_g)], axis=-1)
    x = x + g1[:, None] * (mixed @ w_out)
    h2 = modulate(rmsnorm(x, norm2_g), sh2, sc2)
    x = x + g2[:, None] * peer(h2.reshape(B * T, D), peer_wq, peer_keys, peer_u, peer_v).reshape(B, T, D)
    return (x, conv_cat[:, -(CONV_WIDTH - 1):], h_last.astype(h_prev.dtype), pool_cat[:, -POOL_HIST:])


def setup_inputs(seed: int = 0) -> dict:
    key = jax.random.key(seed)
    ks = jax.random.split(key, 32)

    def nrm(k, shape, scale):
        return jax.random.normal(k, shape, jnp.float32) * scale

    u = jax.random.uniform(ks[14], (DEPTH, LRU_WIDTH), jnp.float32, minval=0.9, maxval=0.999)
    a0 = u ** (1.0 / LRU_C)
    lru_lambda = jnp.log(a0) - jnp.log1p(-a0)
    return {
        'x_prompt': nrm(ks[0], (BATCH, SEQ, D_MODEL), 1.0),
        'x_sample': nrm(ks[1], (DEC_BATCH, DEC_SEQ, D_MODEL), 1.0),
        'c_prompt': nrm(ks[2], (BATCH, D_MODEL), 1.0),
        'c_sample': nrm(ks[3], (DEC_BATCH, D_MODEL), 1.0),
        'state_conv': nrm(ks[4], (DEPTH, DEC_BATCH, CONV_WIDTH - 1, LRU_WIDTH), 1.0),
        'state_lru': nrm(ks[5], (DEPTH, DEC_BATCH, LRU_WIDTH), 0.5),
        'state_pool': nrm(ks[6], (DEPTH, DEC_BATCH, POOL_HIST, POOL_WIDTH), 1.0),
        'w_ada': nrm(ks[7], (DEPTH, D_MODEL, N_MOD * D_MODEL), 0.5 * D_MODEL ** -0.5),
        'b_ada': nrm(ks[8], (DEPTH, N_MOD * D_MODEL), 0.02),
        'norm1_g': 1.0 + nrm(ks[9], (DEPTH, D_MODEL), 0.02),
        'w_in': nrm(ks[10], (DEPTH, D_MODEL, 2 * LRU_WIDTH + POOL_WIDTH), D_MODEL ** -0.5),
        'conv_w': nrm(ks[11], (DEPTH, CONV_WIDTH, LRU_WIDTH), CONV_WIDTH ** -0.5),
        'conv_b': nrm(ks[12], (DEPTH, LRU_WIDTH), 0.02),
        'lru_wa': nrm(ks[13], (DEPTH, LRU_HEADS, LRU_HEAD_DIM, LRU_HEAD_DIM), LRU_HEAD_DIM ** -0.5),
        'lru_ba': nrm(ks[15], (DEPTH, LRU_HEADS, LRU_HEAD_DIM), 0.02),
        'lru_wx': nrm(ks[16], (DEPTH, LRU_HEADS, LRU_HEAD_DIM, LRU_HEAD_DIM), LRU_HEAD_DIM ** -0.5),
        'lru_bx': nrm(ks[17], (DEPTH, LRU_HEADS, LRU_HEAD_DIM), 0.02),
        'lru_lambda': lru_lambda,
        'pool_w': nrm(ks[18], (DEPTH, POOL_GROUPS, POOL_GROUP_DIM, POOL_GROUP_DIM), POOL_GROUP_DIM ** -0.5),
        'pool_b': nrm(ks[19], (DEPTH, POOL_GROUPS, POOL_GROUP_DIM), 0.02),
        'pool_scale': 1.0 + nrm(ks[20], (DEPTH, POOL_WIDTH), 0.1),
        'gnorm_lru_g': 1.0 + nrm(ks[21], (DEPTH, LRU_WIDTH), 0.02),
        'gnorm_pool_g': 1.0 + nrm(ks[22], (DEPTH, POOL_WIDTH), 0.02),
        'w_out': nrm(ks[23], (DEPTH, MIX_WIDTH, D_MODEL), MIX_WIDTH ** -0.5),
        'norm2_g': 1.0 + nrm(ks[24], (DEPTH, D_MODEL), 0.02),
        'peer_wq': nrm(ks[25], (DEPTH, D_MODEL, PEER_HEADS * PEER_DKEY), D_MODEL ** -0.5),
        'peer_keys': nrm(ks[26], (DEPTH, PEER_HEADS, 2, PEER_NKEYS, PEER_DHALF), PEER_DHALF ** -0.5),
        'peer_u': nrm(ks[27], (DEPTH, PEER_EXPERTS, D_MODEL), D_MODEL ** -0.5),
        'peer_v': nrm(ks[28], (DEPTH, PEER_EXPERTS, D_MODEL), PEER_HEADS ** -0.5),
        'final_g': 1.0 + nrm(ks[29], (D_MODEL,), 0.02),
    }


def reference(x_prompt, x_sample, c_prompt, c_sample, state_conv, state_lru, state_pool,
              w_ada, b_ada, norm1_g, w_in, conv_w, conv_b, lru_wa, lru_ba, lru_wx, lru_bx,
              lru_lambda, pool_w, pool_b, pool_scale, gnorm_lru_g, gnorm_pool_g, w_out,
              norm2_g, peer_wq, peer_keys, peer_u, peer_v, final_g):
    dt = x_prompt.dtype
    bp = x_prompt.shape[0]
    xp, xs = x_prompt, x_sample
    conv_p, lru_p, pool_p, conv_s, lru_s, pool_s = [], [], [], [], [], []
    for l in range(DEPTH):
        params = (w_ada[l], b_ada[l], norm1_g[l], w_in[l], conv_w[l], conv_b[l], lru_wa[l], lru_ba[l],
                  lru_wx[l], lru_bx[l], lru_lambda[l], pool_w[l], pool_b[l], pool_scale[l],
                  gnorm_lru_g[l], gnorm_pool_g[l], w_out[l], norm2_g[l], peer_wq[l], peer_keys[l],
                  peer_u[l], peer_v[l])
        xp, cp, hp, pp = decoder_layer(
            xp, c_prompt,
            jnp.zeros((bp, CONV_WIDTH - 1, LRU_WIDTH), dt),
            jnp.zeros((bp, LRU_WIDTH), dt),
            jnp.zeros((bp, POOL_HIST, POOL_WIDTH), dt),
            0, *params)
        xs, cs, hs, ps = decoder_layer(xs, c_sample, state_conv[l], state_lru[l], state_pool[l], PAST_LEN, *params)
        conv_p.append(cp); lru_p.append(hp); pool_p.append(pp)
        conv_s.append(cs); lru_s.append(hs); pool_s.append(ps)
    y_prompt = rmsnorm(xp, final_g)
    y_sample = rmsnorm(xs, final_g)
    return (y_prompt, y_sample, jnp.stack(conv_p), jnp.stack(lru_p), jnp.stack(pool_p),
            jnp.stack(conv_s), jnp.stack(lru_s), jnp.stack(pool_s))
```

```python
import functools

import jax
import jax.numpy as jnp
from jax import lax
from jax.experimental import pallas as pl
from jax.experimental.pallas import tpu as pltpu

F32 = jnp.float32
BF16 = jnp.bfloat16

LRU_HEADS = 8
CONV_WIDTH = 4
LRU_C = 8.0
POOL_WINDOWS = (2, 4, 8, 16)
POOL_HIST = max(POOL_WINDOWS) - 1
PEER_HEADS = 8
PEER_NKEYS = 128
PEER_TOPK = 16
N_MOD = 6
EPS = 1e-6
PAST_LEN = 16384

LANES = 128
SUBLANES = 8
VMEM_LIMIT_BYTES = 56 * 1024 * 1024

ADA_TN = 1024
MIX_TT = 256
ROW_TILE = 512
SCORE_TT = 256
PEER_TT = 512
PEER_EC = 1024


def _full_spec(shape):
    nd = len(shape)
    return pl.BlockSpec(shape, lambda *_: (0,) * nd)


def _params(sem):
    return pltpu.CompilerParams(dimension_semantics=sem, vmem_limit_bytes=VMEM_LIMIT_BYTES)


def _rms(x, g):
    ms = jnp.mean(x * x, axis=-1, keepdims=True)
    return x * lax.rsqrt(ms + EPS) * g


def _dot(a, b):
    return jnp.dot(a, b, preferred_element_type=F32)


def _ada_kernel(c_ref, w_ref, b_ref, o_ref):
    c = c_ref[...]
    s = (c * jax.nn.sigmoid(c)).astype(BF16)
    o_ref[...] = _dot(s, w_ref[...].astype(BF16)) + b_ref[...]


def _ada(c_all, w_ada, b_ada):
    rows, d = c_all.shape
    n = w_ada.shape[1]
    return pl.pallas_call(
        _ada_kernel,
        grid=(n // ADA_TN,),
        in_specs=[
            _full_spec((rows, d)),
            pl.BlockSpec((d, ADA_TN), lambda j: (0, j)),
            pl.BlockSpec((1, ADA_TN), lambda j: (0, j)),
        ],
        out_specs=pl.BlockSpec((rows, ADA_TN), lambda j: (0, j)),
        out_shape=jax.ShapeDtypeStruct((rows, n), F32),
        compiler_params=_params(("arbitrary",)),
        name="ada",
    )(c_all, w_ada, b_ada.reshape(1, n))


def _lru_gates(xc, wg_ref, ba, bx, lam):
    hd = xc.shape[1] // LRU_HEADS
    xcb = xc.astype(BF16)
    rs, gs = [], []
    for h in range(LRU_HEADS):
        g = _dot(xcb[:, h * hd:(h + 1) * hd], wg_ref[h])
        rs.append(g[:, :hd])
        gs.append(g[:, hd:])
    r = jax.nn.sigmoid(jnp.concatenate(rs, axis=1) + ba)
    ig = jax.nn.sigmoid(jnp.concatenate(gs, axis=1) + bx)
    softplus_neg_lam = jnp.maximum(-lam, 0.0) + jnp.log(1.0 + jnp.exp(-jnp.abs(lam)))
    log_a = (-LRU_C) * r * softplus_neg_lam
    a = jnp.exp(log_a)
    mult = jnp.sqrt(-jnp.tanh(log_a) * (a * a + 1.0))
    return a, ig, mult


def _pool_project(d, pw_ref, pb, ps):
    gd = d.shape[1] // len(POOL_WINDOWS)
    outs = []
    for g in range(len(POOL_WINDOWS)):
        cols = slice(g * gd, (g + 1) * gd)
        y = _dot(d[:, cols].astype(BF16), pw_ref[g]) + pb[:, cols]
        outs.append(y * ps[:, cols])
    return jnp.concatenate(outs, axis=1)


def _mix_prompt_kernel(x_ref, sh_ref, sc_ref, n1_ref, win_ref, cw_ref, cb_ref, wg_ref, ba_ref,
                       bx_ref, lam_ref, pw_ref, pb_ref, ps_ref, gl_ref, gp_ref,
                       mixed_ref, nconv_ref, nlru_ref, npool_ref,
                       conv_buf, pool_buf, h_carry, *, tt, cw, pw):
    i = pl.program_id(1)
    hist_c = SUBLANES
    hist_p = 2 * SUBLANES

    @pl.when(i == 0)
    def _():
        conv_buf[0:hist_c, :] = jnp.zeros((hist_c, cw), F32)
        pool_buf[0:hist_p, :] = jnp.zeros((hist_p, pw), F32)
        h_carry[...] = jnp.zeros(h_carry.shape, F32)

    x = x_ref[0]
    h = _rms(x, n1_ref[...]) * (1.0 + sc_ref[0]) + sh_ref[0]
    z = _dot(h.astype(BF16), win_ref[...])
    u_lru = z[:, :cw]
    u_gate = z[:, cw:2 * cw]
    u_pool = z[:, 2 * cw:]

    conv_buf[hist_c:hist_c + tt, :] = u_lru
    xc = u_lru * cw_ref[CONV_WIDTH - 1:CONV_WIDTH, :] + cb_ref[...]
    for k in range(CONV_WIDTH - 1):
        off = hist_c - (CONV_WIDTH - 1 - k)
        xc = xc + conv_buf[off:off + tt, :] * cw_ref[k:k + 1, :]

    a, ig, mult = _lru_gates(xc, wg_ref, ba_ref[...], bx_ref[...], lam_ref[...])
    row = lax.broadcasted_iota(jnp.int32, (tt, cw), 0)
    mult = jnp.where(row + i * tt == 0, 1.0, mult)
    nx = xc * ig * mult

    aa, xx = a, nx
    s = 1
    while s < tt:
        valid = row >= s
        x_sh = jnp.where(valid, pltpu.roll(xx, s, 0), 0.0)
        a_sh = jnp.where(valid, pltpu.roll(aa, s, 0), 1.0)
        xx = xx + aa * x_sh
        aa = aa * a_sh
        s *= 2
    hs = xx + aa * h_carry[0:1, :]
    h_carry[0:1, :] = hs[tt - 1:tt, :]

    y_lru = hs * jax.nn.gelu(u_gate)

    pool_buf[hist_p:hist_p + tt, :] = u_pool
    gd = pw // len(POOL_WINDOWS)
    prow = lax.broadcasted_iota(jnp.int32, (tt, gd), 0) + i * tt
    ds = []
    for g, win in enumerate(POOL_WINDOWS):
        cols = slice(g * gd, (g + 1) * gd)
        acc = u_pool[:, cols]
        for k in range(1, win):
            acc = acc + pool_buf[hist_p - k:hist_p - k + tt, cols]
        cnt = jnp.minimum(prow + 1, win).astype(F32)
        ds.append(acc / cnt - u_pool[:, cols])
    y_pool = _pool_project(jnp.concatenate(ds, axis=1), pw_ref, pb_ref[...], ps_ref[...])

    mixed_ref[0, :, :cw] = _rms(y_lru, gl_ref[...]).astype(BF16)
    mixed_ref[0, :, cw:] = _rms(y_pool, gp_ref[...]).astype(BF16)

    tail_c = conv_buf[tt:tt + hist_c, :]
    tail_p = pool_buf[tt:tt + hist_p, :]
    nconv_ref[0] = tail_c
    npool_ref[0] = tail_p
    nlru_ref[0] = hs[tt - SUBLANES:tt, :]
    conv_buf[0:hist_c, :] = tail_c
    pool_buf[0:hist_p, :] = tail_p


def _mix_prompt(x, sh1, sc1, lw):
    b, t, d = x.shape
    cw = lw["conv_w"].shape[1]
    pw = lw["pool_scale"].shape[1]
    tt = MIX_TT
    kern = functools.partial(_mix_prompt_kernel, tt=tt, cw=cw, pw=pw)
    bspec = lambda shape: pl.BlockSpec(shape, lambda bi, ti: (bi, 0, 0))
    weights = [lw["norm1_g"], lw["w_in"], lw["conv_w"], lw["conv_b"], lw["w_gate"], lw["lru_ba"],
               lw["lru_bx"], lw["lru_lambda"], lw["pool_w"], lw["pool_b"], lw["pool_scale"],
               lw["gnorm_lru_g"], lw["gnorm_pool_g"]]
    return pl.pallas_call(
        kern,
        grid=(b, t // tt),
        in_specs=[pl.BlockSpec((1, tt, d), lambda bi, ti: (bi, ti, 0)),
                  bspec((1, 1, d)), bspec((1, 1, d))] + [_full_spec(w.shape) for w in weights],
        out_specs=[pl.BlockSpec((1, tt, cw + pw), lambda bi, ti: (bi, ti, 0)),
                   bspec((1, SUBLANES, cw)), bspec((1, SUBLANES, cw)), bspec((1, 2 * SUBLANES, pw))],
        out_shape=[jax.ShapeDtypeStruct((b, t, cw + pw), BF16),
                   jax.ShapeDtypeStruct((b, SUBLANES, cw), F32),
                   jax.ShapeDtypeStruct((b, SUBLANES, cw), F32),
                   jax.ShapeDtypeStruct((b, 2 * SUBLANES, pw), F32)],
        scratch_shapes=[pltpu.VMEM((tt + SUBLANES, cw), F32),
                        pltpu.VMEM((tt + 2 * SUBLANES, pw), F32),
                        pltpu.VMEM((SUBLANES, cw), F32)],
        compiler_params=_params(("arbitrary", "arbitrary")),
        name="mix_prompt",
    )(x, sh1, sc1, *weights)


def _mix_sample_kernel(x_ref, sh_ref, sc_ref, sconv_ref, slru_ref, spool_ref, n1_ref, win_ref, cw_ref,
                       cb_ref, wg_ref, ba_ref, bx_ref, lam_ref, pw_ref, pb_ref, ps_ref, gl_ref, gp_ref,
                       mixed_ref, ulru_ref, upool_ref, hlast_ref, *, nt, nb, cw, pw, start_pos):
    n1 = n1_ref[...]
    hs_in = []
    for t in range(nt):
        hs_in.append(_rms(x_ref[t], n1) * (1.0 + sc_ref[...]) + sh_ref[...])
    h = jnp.concatenate(hs_in, axis=0)
    z = _dot(h.astype(BF16), win_ref[...])
    u_lru = z[:, :cw]
    u_gate = z[:, cw:2 * cw]
    u_pool = z[:, 2 * cw:]

    rows = lambda arr, t: arr[t * nb:(t + 1) * nb, :]
    conv_cat = [sconv_ref[k] for k in range(CONV_WIDTH - 1)] + [rows(u_lru, t) for t in range(nt)]
    xcs = []
    for t in range(nt):
        acc = conv_cat[t + CONV_WIDTH - 1] * cw_ref[CONV_WIDTH - 1:CONV_WIDTH, :] + cb_ref[...]
        for k in range(CONV_WIDTH - 1):
            acc = acc + conv_cat[t + k] * cw_ref[k:k + 1, :]
        xcs.append(acc)
    xc = jnp.concatenate(xcs, axis=0)

    a, ig, mult = _lru_gates(xc, wg_ref, ba_ref[...], bx_ref[...], lam_ref[...])
    hcur = slru_ref[...]
    hs = []
    for t in range(nt):
        m_t = rows(mult, t)
        if start_pos + t == 0:
            m_t = jnp.ones_like(m_t)
        hcur = rows(a, t) * hcur + rows(xc, t) * rows(ig, t) * m_t
        hs.append(hcur)
    hlast_ref[...] = hcur
    y_lru = jnp.concatenate(hs, axis=0) * jax.nn.gelu(u_gate)

    pool_cat = [spool_ref[k] for k in range(POOL_HIST)] + [rows(u_pool, t) for t in range(nt)]
    gd = pw // len(POOL_WINDOWS)
    drows = []
    for t in range(nt):
        parts = []
        for g, win in enumerate(POOL_WINDOWS):
            cols = slice(g * gd, (g + 1) * gd)
            acc = pool_cat[POOL_HIST + t][:, cols]
            for k in range(1, win):
                acc = acc + pool_cat[POOL_HIST + t - k][:, cols]
            cnt = float(min(start_pos + t + 1, win))
            parts.append(acc / cnt - pool_cat[POOL_HIST + t][:, cols])
        drows.append(jnp.concatenate(parts, axis=1))
    y_pool = _pool_project(jnp.concatenate(drows, axis=0), pw_ref, pb_ref[...], ps_ref[...])

    mixed_ref[:, :cw] = _rms(y_lru, gl_ref[...]).astype(BF16)
    mixed_ref[:, cw:] = _rms(y_pool, gp_ref[...]).astype(BF16)
    for t in range(nt):
        ulru_ref[t] = rows(u_lru, t)
        upool_ref[t] = rows(u_pool, t)


def _mix_sample(x_tm, sh1, sc1, sconv_tm, slru, spool_tm, lw, start_pos):
    nt, nb, d = x_tm.shape
    cw = lw["conv_w"].shape[1]
    pw = lw["pool_scale"].shape[1]
    kern = functools.partial(_mix_sample_kernel, nt=nt, nb=nb, cw=cw, pw=pw, start_pos=start_pos)
    weights = [lw["norm1_g"], lw["w_in"], lw["conv_w"], lw["conv_b"], lw["w_gate"], lw["lru_ba"],
               lw["lru_bx"], lw["lru_lambda"], lw["pool_w"], lw["pool_b"], lw["pool_scale"],
               lw["gnorm_lru_g"], lw["gnorm_pool_g"]]
    ins = [x_tm, sh1, sc1, sconv_tm, slru, spool_tm] + weights
    return pl.pallas_call(
        kern,
        grid=(1,),
        in_specs=[_full_spec(v.shape) for v in ins],
        out_specs=[_full_spec((nt * nb, cw + pw)), _full_spec((nt, nb, cw)),
                   _full_spec((nt, nb, pw)), _full_spec((nb, cw))],
        out_shape=[jax.ShapeDtypeStruct((nt * nb, cw + pw), BF16),
                   jax.ShapeDtypeStruct((nt, nb, cw), F32),
                   jax.ShapeDtypeStruct((nt, nb, pw), F32),
                   jax.ShapeDtypeStruct((nb, cw), F32)],
        compiler_params=_params(("arbitrary",)),
        name="mix_sample",
    )(*ins)


def _outproj_kernel(x_ref, m_ref, g1_ref, sh_ref, sc_ref, wout_ref, n2_ref, x1_ref, h2t_ref):
    x1 = x_ref[...] + g1_ref[0] * _dot(m_ref[...], wout_ref[...])
    x1_ref[...] = x1
    h2 = _rms(x1, n2_ref[...]) * (1.0 + sc_ref[0]) + sh_ref[0]
    h2t_ref[...] = h2.T.astype(BF16)


def _outproj(x2d, mixed2d, g1, sh2, sc2, lw, rows_per_mod):
    r, d = x2d.shape
    tile = ROW_TILE
    per = rows_per_mod // tile
    mspec = pl.BlockSpec((1,) + g1.shape[1:], lambda i: (i // per, 0, 0))
    return pl.pallas_call(
        _outproj_kernel,
        grid=(r // tile,),
        in_specs=[pl.BlockSpec((tile, d), lambda i: (i, 0)),
                  pl.BlockSpec((tile, mixed2d.shape[1]), lambda i: (i, 0)),
                  mspec, mspec, mspec,
                  _full_spec(lw["w_out"].shape), _full_spec(lw["norm2_g"].shape)],
        out_specs=[pl.BlockSpec((tile, d), lambda i: (i, 0)),
                   pl.BlockSpec((d, tile), lambda i: (0, i))],
        out_shape=[jax.ShapeDtypeStruct((r, d), F32), jax.ShapeDtypeStruct((d, r), BF16)],
        compiler_params=_params(("arbitrary",)),
        name="outproj",
    )(x2d, mixed2d, g1, sh2, sc2, lw["w_out"], lw["norm2_g"])


def _top_values(s, n):
    outs = []
    for _ in range(n):
        m = jnp.max(s, axis=0, keepdims=True)
        outs.append(m)
        s = jnp.where(s == m, -jnp.inf, s)
    return outs, s


_CAND_PAIRS = [(a, b) for a in range(PEER_TOPK + 1) for b in range(PEER_TOPK + 1)
               if (a + 1) * (b + 1) <= PEER_TOPK + 1]


def _scores_kernel(h2t_ref, wqt_ref, keys_ref, s2_ref, e2_ref, th_ref, e1_ref, *, dh):
    qt = _dot(wqt_ref[...], h2t_ref[...])
    tp = qt.shape[1]
    for h in range(PEER_HEADS):
        q1 = qt[(2 * h) * dh:(2 * h + 1) * dh, :].astype(BF16)
        q2 = qt[(2 * h + 1) * dh:(2 * h + 2) * dh, :].astype(BF16)
        s1 = _dot(keys_ref[2 * h], q1)
        s2 = _dot(keys_ref[2 * h + 1], q2)
        t1, _ = _top_values(s1, PEER_TOPK + 1)
        t2, _ = _top_values(s2, PEER_TOPK + 1)
        cand_rows = [t1[a] + t2[b] for a, b in _CAND_PAIRS]
        pad = (-len(cand_rows)) % SUBLANES
        cand_rows += [jnp.full((1, tp), -jnp.inf, F32)] * pad
        cand = jnp.concatenate(cand_rows, axis=0)
        top, rest = _top_values(cand, PEER_TOPK)
        c17 = jnp.max(rest, axis=0, keepdims=True)
        tau = 0.5 * (top[PEER_TOPK - 1] + c17)
        cmax = top[0]
        z = jnp.exp(top[0] - cmax)
        for k in range(1, PEER_TOPK):
            z = z + jnp.exp(top[k] - cmax)
        s2_ref[h] = s2
        e2_ref[h] = jnp.exp(s2 - t2[0])
        th_ref[h] = tau - s1
        e1_ref[h] = jnp.exp(s1 - t1[0]) / z


def _scores(h2t, lw):
    d, t = h2t.shape
    tp = SCORE_TT
    nk, dh = lw["keys"].shape[1:]
    kern = functools.partial(_scores_kernel, dh=dh)
    out = jax.ShapeDtypeStruct((PEER_HEADS, nk, t), F32)
    ospec = pl.BlockSpec((PEER_HEADS, nk, tp), lambda i: (0, 0, i))
    return pl.pallas_call(
        kern,
        grid=(t // tp,),
        in_specs=[pl.BlockSpec((d, tp), lambda i: (0, i)),
                  _full_spec(lw["wq_t"].shape), _full_spec(lw["keys"].shape)],
        out_specs=[ospec] * 4,
        out_shape=[out] * 4,
        compiler_params=_params(("arbitrary",)),
        name="scores",
    )(h2t, lw["wq_t"], lw["keys"])


def _peer_kernel(h2t_ref, eu_ref, evt_ref, s2_ref, e2_ref, th_ref, e1_ref, out_ref, a_buf, p_buf,
                 *, tt, ec, nk):
    c = pl.program_id(1)
    a_buf[...] = _dot(eu_ref[...], h2t_ref[...])
    for tg in range(tt // LANES):
        lanes = slice(tg * LANES, (tg + 1) * LANES)
        for j in range(ec // nk):
            w = jnp.zeros((nk, LANES), F32)
            for h in range(PEER_HEADS):
                th = jnp.broadcast_to(th_ref[h, 0, j:j + 1, lanes], (nk, LANES))
                e1 = jnp.broadcast_to(e1_ref[h, 0, j:j + 1, lanes], (nk, LANES))
                w = w + e1 * jnp.where(s2_ref[h, :, lanes] >= th, e2_ref[h, :, lanes], 0.0)
            rows = slice(j * nk, (j + 1) * nk)
            p_buf[rows, lanes] = (w * jax.nn.gelu(a_buf[rows, lanes])).astype(BF16)
    contrib = _dot(evt_ref[...], p_buf[...])

    @pl.when(c == 0)
    def _():
        out_ref[...] = contrib

    @pl.when(c > 0)
    def _():
        out_ref[...] += contrib


def _peer(h2t, s2, e2, th, e1, lw):
    d, t = h2t.shape
    ne = lw["eu"].shape[0]
    nk = s2.shape[1]
    tt, ec = PEER_TT, PEER_EC
    per = ec // nk
    th4 = th.reshape(PEER_HEADS, nk // per, per, t)
    e14 = e1.reshape(PEER_HEADS, nk // per, per, t)
    kern = functools.partial(_peer_kernel, tt=tt, ec=ec, nk=nk)
    tok = pl.BlockSpec((PEER_HEADS, nk, tt), lambda ti, ci: (0, 0, ti))
    chk = pl.BlockSpec((PEER_HEADS, 1, per, tt), lambda ti, ci: (0, ci, 0, ti))
    return pl.pallas_call(
        kern,
        grid=(t // tt, ne // ec),
        in_specs=[pl.BlockSpec((d, tt), lambda ti, ci: (0, ti)),
                  pl.BlockSpec((ec, d), lambda ti, ci: (ci, 0)),
                  pl.BlockSpec((d, ec), lambda ti, ci: (0, ci)),
                  tok, tok, chk, chk],
        out_specs=pl.BlockSpec((d, tt), lambda ti, ci: (0, ti)),
        out_shape=jax.ShapeDtypeStruct((d, t), F32),
        scratch_shapes=[pltpu.VMEM((ec, tt), F32), pltpu.VMEM((ec, tt), BF16)],
        compiler_params=_params(("arbitrary", "arbitrary")),
        name="peer",
    )(h2t, lw["eu"], lw["ev_t"], s2, e2, th4, e14)


def _final_kernel(x1_ref, pt_ref, g2_ref, fg_ref, y_ref):
    x2 = x1_ref[...] + g2_ref[0] * pt_ref[...].T
    y_ref[...] = _rms(x2, fg_ref[...])


def _final(x1, peer_t, g2, final_g, rows_per_mod):
    r, d = x1.shape
    tile = ROW_TILE
    per = rows_per_mod // tile
    return pl.pallas_call(
        _final_kernel,
        grid=(r // tile,),
        in_specs=[pl.BlockSpec((tile, d), lambda i: (i, 0)),
                  pl.BlockSpec((d, tile), lambda i: (0, i)),
                  pl.BlockSpec((1,) + g2.shape[1:], lambda i: (i // per, 0, 0)),
                  _full_spec(final_g.shape)],
        out_specs=pl.BlockSpec((tile, d), lambda i: (i, 0)),
        out_shape=jax.ShapeDtypeStruct((r, d), F32),
        compiler_params=_params(("arbitrary",)),
        name="final",
    )(x1, peer_t, g2, final_g)


def _channel_mix(x2d, mixed2d, mods, lw, rows_per_mod):
    g1, sh2, sc2, g2 = mods
    x1, h2t = _outproj(x2d, mixed2d, g1, sh2, sc2, lw, rows_per_mod)
    s2, e2, th, e1 = _scores(h2t, lw)
    peer_t = _peer(h2t, s2, e2, th, e1, lw)
    return x1, peer_t, g2


def kernel(x_prompt, x_sample, c_prompt, c_sample, state_conv, state_lru, state_pool, w_ada, b_ada, norm1_g, w_in, conv_w, conv_b, lru_wa, lru_ba, lru_wx, lru_bx, lru_lambda, pool_w, pool_b, pool_scale, gnorm_lru_g, gnorm_pool_g, w_out, norm2_g, peer_wq, peer_keys, peer_u, peer_v, final_g):
    depth = w_ada.shape[0]
    assert depth == 1, "the final norm is fused into the layer's last kernel"
    bp, t, d = x_prompt.shape
    bs, ts, _ = x_sample.shape
    cw = conv_w.shape[-1]
    pw = pool_scale.shape[-1]
    assert t % MIX_TT == 0 and t % ROW_TILE == 0 and (bs * ts) % ROW_TILE == 0
    assert (bp * t) % PEER_TT == 0 and (bs * ts) % PEER_TT == 0 and MIX_TT >= POOL_HIST + 1

    c_all = jnp.concatenate([c_prompt, c_sample], axis=0)
    pad = (-c_all.shape[0]) % (2 * SUBLANES)
    c_all = jnp.pad(c_all, ((0, pad), (0, 0)))

    xp = x_prompt
    xs_tm = x_sample.transpose(1, 0, 2)
    outs = {k: [] for k in ("cp", "hp", "pp", "cs", "hs", "ps")}
    final2d = final_g.reshape(1, d)
    for l in range(depth):
        row = lambda v: v[l].reshape(1, -1)
        lw = {
            "norm1_g": row(norm1_g), "w_in": w_in[l].astype(BF16), "conv_w": conv_w[l],
            "conv_b": row(conv_b),
            "w_gate": jnp.concatenate([lru_wa[l], lru_wx[l]], axis=-1).astype(BF16),
            "lru_ba": row(lru_ba), "lru_bx": row(lru_bx), "lru_lambda": row(lru_lambda),
            "pool_w": pool_w[l].astype(BF16), "pool_b": row(pool_b), "pool_scale": row(pool_scale),
            "gnorm_lru_g": row(gnorm_lru_g), "gnorm_pool_g": row(gnorm_pool_g),
            "w_out": w_out[l].astype(BF16), "norm2_g": row(norm2_g),
            "wq_t": peer_wq[l].T.astype(BF16),
            "keys": peer_keys[l].reshape(2 * PEER_HEADS, PEER_NKEYS, -1).astype(BF16),
            "eu": peer_u[l].astype(BF16), "ev_t": peer_v[l].T.astype(BF16),
        }
        mod = _ada(c_all, w_ada[l], b_ada[l])
        sh1, sc1, g1, sh2, sc2, g2 = jnp.split(mod, N_MOD, axis=-1)
        pm = lambda v: v[:bp].reshape(bp, 1, d)
        sm = lambda v: v[bp:bp + bs]
        smt = lambda v: jnp.tile(sm(v), (ts, 1)).reshape(1, ts * bs, d)

        mixed_p, nconv, nlru, npool = _mix_prompt(xp, pm(sh1), pm(sc1), lw)
        x1p, peer_p, g2p = _channel_mix(xp.reshape(bp * t, d), mixed_p.reshape(bp * t, cw + pw),
                                        (pm(g1), pm(sh2), pm(sc2), pm(g2)), lw, t)
        outs["cp"].append(nconv[:, SUBLANES - (CONV_WIDTH - 1):, :])
        outs["hp"].append(nlru[:, SUBLANES - 1, :])
        outs["pp"].append(npool[:, 2 * SUBLANES - POOL_HIST:, :])

        mixed_s, ulru, upool, hlast = _mix_sample(
            xs_tm, sm(sh1), sm(sc1), state_conv[l].transpose(1, 0, 2), state_lru[l],
            state_pool[l].transpose(1, 0, 2), lw, PAST_LEN)
        x1s, peer_s, g2s = _channel_mix(xs_tm.reshape(ts * bs, d), mixed_s,
                                        (smt(g1), smt(sh2), smt(sc2), smt(g2)), lw, ts * bs)
        conv_cat = jnp.concatenate([state_conv[l], ulru.transpose(1, 0, 2)], axis=1)
        pool_cat = jnp.concatenate([state_pool[l], upool.transpose(1, 0, 2)], axis=1)
        outs["cs"].append(conv_cat[:, -(CONV_WIDTH - 1):])
        outs["hs"].append(hlast)
        outs["ps"].append(pool_cat[:, -POOL_HIST:])

        yp = _final(x1p, peer_p, g2p, final2d, t)
        ys = _final(x1s, peer_s, g2s, final2d, ts * bs)

    y_prompt = yp.reshape(bp, t, d)
    y_sample = ys.reshape(ts, bs, d).transpose(1, 0, 2)
    st = lambda k: jnp.stack(outs[k])
    return (y_prompt, y_sample, st("cp"), st("hp"), st("pp"), st("cs"), st("hs"), st("ps"))
```
